```python
import jax, jax.numpy as jnp
from jax import lax
import numpy as np

D_MODEL = 1024
BATCH = 8
SEQ = 2048
DEPTH = 4

CHUNK = 64
N_MIXERS = 2
N_ATTN_LAYERS = (DEPTH + 1) // 2
N_REC_LAYERS = DEPTH // 2

N_HEADS = 16
HEAD_DIM = D_MODEL // N_HEADS
LEFT_CHUNKS = 8
BAND_CHUNKS = LEFT_CHUNKS + 1
BAND = BAND_CHUNKS * CHUNK
MAX_REL_DIST = 128
N_REL = 2 * MAX_REL_DIST + 1
NEG_INF = -1e30

D_RNN = D_MODEL
N_RG_BLOCKS = 8
RG_BLOCK = D_RNN // N_RG_BLOCKS
CONV_WIDTH = 4
RG_C = 8.0

D_FF = 4 * D_MODEL
RMS_EPS = 1e-6

kernel_name = "hybrid_chunk_attn_rglru_trunk"


def rmsnorm(x, g):
    xf = x.astype(jnp.float32)
    y = xf * lax.rsqrt(jnp.mean(xf * xf, axis=-1, keepdims=True) + RMS_EPS)
    return (y * g.astype(jnp.float32)).astype(x.dtype)


def chunked_band_attention(h, w_qkv, w_o, rel_table):
    B, S, _ = h.shape
    n_chunks = S // CHUNK
    qkv = jnp.einsum('bsd,de->bse', h, w_qkv).reshape(B, S, 3, N_HEADS, HEAD_DIM)
    q, k, v = qkv[:, :, 0], qkv[:, :, 1], qkv[:, :, 2]
    pad = LEFT_CHUNKS * CHUNK
    k_pad = jnp.pad(k, ((0, 0), (pad, 0), (0, 0), (0, 0)))
    v_pad = jnp.pad(v, ((0, 0), (pad, 0), (0, 0), (0, 0)))
    q_chunks = q.reshape(B, n_chunks, CHUNK, N_HEADS, HEAD_DIM).transpose(1, 0, 2, 3, 4)
    qi = jnp.arange(CHUNK)[:, None]
    kj = jnp.arange(BAND)[None, :]
    rel = qi + pad - kj
    idx = jnp.clip(rel, -MAX_REL_DIST, MAX_REL_DIST) + MAX_REL_DIST
    bias = rel_table[:, idx].astype(jnp.float32)
    scale = HEAD_DIM ** -0.5
    band_offsets = jnp.arange(BAND)

    def one_chunk(args):
        c, q_c = args
        start = c * CHUNK
        k_band = lax.dynamic_slice_in_dim(k_pad, start, BAND, axis=1)
        v_band = lax.dynamic_slice_in_dim(v_pad, start, BAND, axis=1)
        s = jnp.einsum('bqhd,bkhd->bhqk', q_c, k_band).astype(jnp.float32) * scale + bias
        key_pos = start - pad + band_offsets
        s = jnp.where(key_pos[None, None, None, :] >= 0, s, NEG_INF)
        p = jax.nn.softmax(s, axis=-1).astype(v_band.dtype)
        return jnp.einsum('bhqk,bkhd->bqhd', p, v_band)

    o = lax.map(one_chunk, (jnp.arange(n_chunks), q_chunks))
    o = o.transpose(1, 0, 2, 3, 4).reshape(B, S, D_MODEL)
    return jnp.einsum('bsd,de->bse', o, w_o)


def rglru_block(h, w_in, b_in, conv_w, conv_b, w_ga, b_ga, w_gx, b_gx, a_param, w_o, b_o):
    B, S, _ = h.shape
    u = jnp.einsum('bsd,de->bse', h, w_in) + b_in
    x_br = u[..., :D_RNN]
    y_br = jax.nn.gelu(u[..., D_RNN:], approximate=True)
    xp = jnp.pad(x_br, ((0, 0), (CONV_WIDTH - 1, 0), (0, 0)))
    xc = conv_b + xp[:, 0:S] * conv_w[0]
    for tap in range(1, CONV_WIDTH):
        xc = xc + xp[:, tap:tap + S] * conv_w[tap]
    xb = xc.reshape(B, S, N_RG_BLOCKS, RG_BLOCK)
    r = jax.nn.sigmoid(jnp.einsum('bsnc,nce->bsne', xb, w_ga) + b_ga).reshape(B, S, D_RNN)
    ig = jax.nn.sigmoid(jnp.einsum('bsnc,nce->bsne', xb, w_gx) + b_gx).reshape(B, S, D_RNN)
    log_a = -RG_C * r.astype(jnp.float32) * jax.nn.softplus(-a_param.astype(jnp.float32))
    a = jnp.exp(log_a)
    mult = jnp.sqrt(-jnp.expm1(2.0 * log_a))
    b_seq = mult * (ig * xc).astype(jnp.float32)

    def combine(left, right):
        a1, b1 = left
        a2, b2 = right
        return a1 * a2, a2 * b1 + b2

    _, hs = lax.associative_scan(combine, (a, b_seq), axis=1)
    out = hs.astype(h.dtype) * y_br
    return jnp.einsum('bse,ed->bsd', out, w_o) + b_o


def sq_relu_mlp(h, w1, w2):
    z = jax.nn.relu(jnp.einsum('bsd,df->bsf', h, w1))
    return jnp.einsum('bsf,fd->bsd', z * z, w2)


def setup_inputs(seed: int = 0) -> dict:
    key = jax.random.key(seed)
    ks = jax.random.split(key, 20)
    nrm = lambda k, shape, s: jax.random.normal(k, shape, jnp.float32) * s
    x = jax.random.normal(ks[0], (BATCH, SEQ, D_MODEL), jnp.float32)
    norm_mix = 1.0 + nrm(ks[1], (DEPTH, D_MODEL), 0.05)
    norm_mlp = 1.0 + nrm(ks[2], (DEPTH, D_MODEL), 0.05)
    attn_w_qkv = nrm(ks[3], (N_ATTN_LAYERS, D_MODEL, 3 * D_MODEL), D_MODEL ** -0.5)
    attn_w_o = nrm(ks[4], (N_ATTN_LAYERS, D_MODEL, D_MODEL), D_MODEL ** -0.5)
    attn_rel_bias = nrm(ks[5], (N_ATTN_LAYERS, N_HEADS, N_REL), 0.2)
    rec_w_in = nrm(ks[6], (N_REC_LAYERS, D_MODEL, 2 * D_RNN), D_MODEL ** -0.5)
    rec_b_in = nrm(ks[7], (N_REC_LAYERS, 2 * D_RNN), 0.02)
    rec_conv_w = nrm(ks[8], (N_REC_LAYERS, CONV_WIDTH, D_RNN), CONV_WIDTH ** -0.5)
    rec_conv_b = nrm(ks[9], (N_REC_LAYERS, D_RNN), 0.02)
    rec_w_ga = nrm(ks[10], (N_REC_LAYERS, N_RG_BLOCKS, RG_BLOCK, RG_BLOCK), RG_BLOCK ** -0.5)
    rec_b_ga = nrm(ks[11], (N_REC_LAYERS, N_RG_BLOCKS, RG_BLOCK), 0.02)
    rec_w_gx = nrm(ks[12], (N_REC_LAYERS, N_RG_BLOCKS, RG_BLOCK, RG_BLOCK), RG_BLOCK ** -0.5)
    rec_b_gx = nrm(ks[13], (N_REC_LAYERS, N_RG_BLOCKS, RG_BLOCK), 0.02)
    a0 = jax.random.uniform(ks[14], (N_REC_LAYERS, D_RNN), jnp.float32, 0.81, 0.998)
    s0 = a0 ** (1.0 / RG_C)
    rec_a_param = jnp.log(s0) - jnp.log1p(-s0)
    rec_w_o = nrm(ks[15], (N_REC_LAYERS, D_RNN, D_MODEL), D_RNN ** -0.5)
    rec_b_o = nrm(ks[16], (N_REC_LAYERS, D_MODEL), 0.02)
    mlp_w1 = nrm(ks[17], (DEPTH, D_MODEL, D_FF), D_MODEL ** -0.5)
    mlp_w2 = nrm(ks[18], (DEPTH, D_FF, D_MODEL), D_FF ** -0.5)
    norm_final = 1.0 + nrm(ks[19], (D_MODEL,), 0.05)
    return {"x": x, "norm_mix": norm_mix, "norm_mlp": norm_mlp,
            "attn_w_qkv": attn_w_qkv, "attn_w_o": attn_w_o, "attn_rel_bias": attn_rel_bias,
            "rec_w_in": rec_w_in, "rec_b_in": rec_b_in, "rec_conv_w": rec_conv_w,
            "rec_conv_b": rec_conv_b, "rec_w_ga": rec_w_ga, "rec_b_ga": rec_b_ga,
            "rec_w_gx": rec_w_gx, "rec_b_gx": rec_b_gx, "rec_a_param": rec_a_param,
            "rec_w_o": rec_w_o, "rec_b_o": rec_b_o,
            "mlp_w1": mlp_w1, "mlp_w2": mlp_w2, "norm_final": norm_final}


def reference(x, norm_mix, norm_mlp, attn_w_qkv, attn_w_o, attn_rel_bias,
              rec_w_in, rec_b_in, rec_conv_w, rec_conv_b, rec_w_ga, rec_b_ga,
              rec_w_gx, rec_b_gx, rec_a_param, rec_w_o, rec_b_o,
              mlp_w1, mlp_w2, norm_final):
    h = x
    for layer in range(DEPTH):
        hn = rmsnorm(h, norm_mix[layer])
        j = layer // N_MIXERS
        if layer % N_MIXERS == 0:
            mix = chunked_band_attention(hn, attn_w_qkv[j], attn_w_o[j], attn_rel_bias[j])
        else:
            mix = rglru_block(hn, rec_w_in[j], rec_b_in[j], rec_conv_w[j], rec_conv_b[j],
                              rec_w_ga[j], rec_b_ga[j], rec_w_gx[j], rec_b_gx[j],
                              rec_a_param[j], rec_w_o[j], rec_b_o[j])
        h = h + mix
        h = h + sq_relu_mlp(rmsnorm(h, norm_mlp[layer]), mlp_w1[layer], mlp_w2[layer])
    return rmsnorm(h, norm_final)
```

```python
import functools

import jax
import jax.numpy as jnp
from jax import lax
from jax.experimental import pallas as pl
from jax.experimental.pallas import tpu as pltpu

D_MODEL = 1024
N_HEADS = 16
HEAD_DIM = D_MODEL // N_HEADS
CHUNK = 64
LEFT_CHUNKS = 8
BAND = (LEFT_CHUNKS + 1) * CHUNK
PAD = LEFT_CHUNKS * CHUNK
MAX_REL_DIST = 128
NEG_INF = -1e30
N_RG_BLOCKS = 8
RG_BLOCK = D_MODEL // N_RG_BLOCKS
CONV_WIDTH = 4
RG_C = 8.0
D_FF = 4 * D_MODEL
RMS_EPS = 1e-6

LANES = 128
SUBLANES = 8
HEADS_PER_VREG = LANES // HEAD_DIM
N_HEAD_PAIRS = N_HEADS // HEADS_PER_VREG
VMEM_LIMIT = 56 * 1024 * 1024

ROW_TILE = 512
CHUNKS_PER_STEP = 4
REC_T = 64
FF_TILE = 1024

BF16 = jnp.bfloat16
F32 = jnp.float32


def _rms(x, g):
    ms = jnp.mean(x * x, axis=-1, keepdims=True)
    return x * lax.rsqrt(ms + RMS_EPS) * g


def _const_spec(shape):
    zeros = (0,) * len(shape)
    return pl.BlockSpec(shape, lambda *_: zeros)


def _params(sem):
    return pltpu.CompilerParams(dimension_semantics=sem, vmem_limit_bytes=VMEM_LIMIT)


def _qkv_kernel(h_ref, g_ref, w_ref, o_ref):
    hn = _rms(h_ref[...], g_ref[...]).astype(BF16)
    for n in range(3):
        cols = slice(n * D_MODEL, (n + 1) * D_MODEL)
        acc = jnp.dot(hn, w_ref[:, cols], preferred_element_type=F32)
        if n == 0:
            acc = acc * (HEAD_DIM ** -0.5)
        o_ref[:, cols] = acc.astype(BF16)


def _qkv(h, g, w):
    t = h.shape[0]
    return pl.pallas_call(
        _qkv_kernel,
        out_shape=jax.ShapeDtypeStruct((t, 3 * D_MODEL), BF16),
        grid=(t // ROW_TILE,),
        in_specs=[pl.BlockSpec((ROW_TILE, D_MODEL), lambda i: (i, 0)),
                  _const_spec((1, D_MODEL)),
                  _const_spec((D_MODEL, 3 * D_MODEL))],
        out_specs=pl.BlockSpec((ROW_TILE, 3 * D_MODEL), lambda i: (i, 0)),
        compiler_params=_params(("parallel",)),
        name="attn_qkv",
    )(h, g, w)


def _attn_kernel(q_ref, k_ref, v_ref, bias_ref, o_ref, kpad, vpad):
    g = pl.program_id(1)

    @pl.when(g == 0)
    def _():
        zeros = jnp.zeros((PAD, D_MODEL), BF16)
        kpad[0:PAD, :] = zeros
        vpad[0:PAD, :] = zeros
        kpad[PAD:, :] = k_ref[...]
        vpad[PAD:, :] = v_ref[...]

    col = lax.broadcasted_iota(jnp.int32, (1, BAND), 1)
    first_head = lax.broadcasted_iota(jnp.int32, (1, LANES), 1) < HEAD_DIM
    mask_a = first_head.astype(F32).astype(BF16)
    mask_b = (1.0 - first_head.astype(F32)).astype(BF16)

    def chunk_body(cc, carry):
        c = g * CHUNKS_PER_STEP + cc
        start = pl.multiple_of(c * CHUNK, CHUNK)
        qrow = pl.multiple_of(cc * CHUNK, CHUNK)
        valid = col >= (LEFT_CHUNKS - c) * CHUNK
        for hp in range(N_HEAD_PAIRS):
            ls = slice(hp * LANES, (hp + 1) * LANES)
            q2 = q_ref[pl.ds(qrow, CHUNK), ls]
            kb = kpad[pl.ds(start, BAND), ls]
            vb = vpad[pl.ds(start, BAND), ls]
            qs = jnp.concatenate([q2 * mask_a, q2 * mask_b], axis=0)
            s = lax.dot_general(qs, kb, (((1,), (1,)), ((), ())),
                                preferred_element_type=F32)
            s = jnp.where(valid, s + bias_ref[hp], NEG_INF)
            m = jnp.max(s, axis=-1, keepdims=True)
            e = jnp.exp(s - m)
            l = jnp.sum(e, axis=-1, keepdims=True)
            o2 = jnp.dot(e.astype(BF16), vb, preferred_element_type=F32)
            o2 = o2 / l
            o = jnp.where(first_head, o2[:CHUNK], o2[CHUNK:])
            o_ref[pl.ds(qrow, CHUNK), ls] = o.astype(BF16)
        return carry

    lax.fori_loop(0, CHUNKS_PER_STEP, chunk_body, 0)


def _attn_core(qkv, bias2, batch, seq):
    rows = CHUNKS_PER_STEP * CHUNK
    steps = seq // rows
    return pl.pallas_call(
        _attn_kernel,
        out_shape=jax.ShapeDtypeStruct((batch * seq, D_MODEL), BF16),
        grid=(batch, steps),
        in_specs=[pl.BlockSpec((rows, D_MODEL), lambda b, g: (b * steps + g, 0)),
                  pl.BlockSpec((seq, D_MODEL), lambda b, g: (b, 1)),
                  pl.BlockSpec((seq, D_MODEL), lambda b, g: (b, 2)),
                  _const_spec((N_HEAD_PAIRS, 2 * CHUNK, BAND))],
        out_specs=pl.BlockSpec((rows, D_MODEL), lambda b, g: (b * steps + g, 0)),
        scratch_shapes=[pltpu.VMEM((PAD + seq, D_MODEL), BF16),
                        pltpu.VMEM((PAD + seq, D_MODEL), BF16)],
        compiler_params=_params(("parallel", "arbitrary")),
        name="attn_core",
    )(qkv, qkv, qkv, bias2)


def _rel_bias(rel_table):
    qi = jnp.arange(CHUNK)[:, None]
    kj = jnp.arange(BAND)[None, :]
    idx = jnp.clip(qi + PAD - kj, -MAX_REL_DIST, MAX_REL_DIST) + MAX_REL_DIST
    bias = rel_table[:, idx].astype(F32)
    return bias.reshape(N_HEAD_PAIRS, HEADS_PER_VREG * CHUNK, BAND)


def _rec_kernel(h_ref, g_ref, win_ref, bin_ref, cw_ref, cb_ref, wg_ref, bg_ref,
                ap_ref, o_ref, xs, a_s, b_s, hs_s, carry):
    nb = h_ref.shape[0]
    rows = nb * REC_T
    i = pl.program_id(0)

    @pl.when(i == 0)
    def _():
        xs[:, 0:SUBLANES, :] = jnp.zeros((nb, SUBLANES, D_MODEL), F32)
        carry[...] = jnp.zeros_like(carry)

    x = h_ref[...].reshape(rows, D_MODEL)
    hn = _rms(x, g_ref[...]).astype(BF16)
    u = jnp.dot(hn, win_ref[...], preferred_element_type=F32) + bin_ref[...]
    y = jax.nn.gelu(u[:, D_MODEL:], approximate=True)

    xs[:, SUBLANES:, :] = u[:, :D_MODEL].reshape(nb, REC_T, D_MODEL)
    base = SUBLANES - (CONV_WIDTH - 1)
    xc = cb_ref[...] + xs[:, base:base + REC_T, :] * cw_ref[0:1, :]
    for tap in range(1, CONV_WIDTH):
        xc = xc + xs[:, base + tap:base + tap + REC_T, :] * cw_ref[tap:tap + 1, :]
    xs[:, 0:SUBLANES, :] = xs[:, REC_T:REC_T + SUBLANES, :]
    xc = xc.reshape(rows, D_MODEL)
    xcb = xc.astype(BF16)

    ap = ap_ref[...]
    neg_sp = -RG_C * (jnp.maximum(-ap, 0.0) + jnp.log1p(jnp.exp(-jnp.abs(ap))))
    for n in range(N_RG_BLOCKS):
        cs = slice(n * RG_BLOCK, (n + 1) * RG_BLOCK)
        gz = jnp.dot(xcb[:, cs], wg_ref[n], preferred_element_type=F32) + bg_ref[n]
        r = jax.nn.sigmoid(gz[:, :RG_BLOCK])
        ig = jax.nn.sigmoid(gz[:, RG_BLOCK:])
        log_a = r * neg_sp[:, cs]
        a = jnp.exp(log_a)
        a_s[n] = a
        b_s[n] = jnp.sqrt(1.0 - a * a) * (ig * xc[:, cs])

    def step(t, hc):
        rows_t = pl.ds(t, nb, stride=REC_T)
        new = []
        for n in range(N_RG_BLOCKS):
            hn_ = a_s[n, rows_t, :] * hc[n] + b_s[n, rows_t, :]
            hs_s[n, rows_t, :] = hn_
            new.append(hn_)
        return tuple(new)

    init = tuple(carry[n] for n in range(N_RG_BLOCKS))
    last = lax.fori_loop(0, REC_T, step, init, unroll=8)
    for n in range(N_RG_BLOCKS):
        carry[n] = last[n]
        cs = slice(n * RG_BLOCK, (n + 1) * RG_BLOCK)
        o_ref[:, :, cs] = (hs_s[n] * y[:, cs]).astype(BF16).reshape(nb, REC_T, RG_BLOCK)


def _rec_front(h3, g, w_in, b_in, conv_w, conv_b, w_gate, b_gate, a_param):
    nb, seq, _ = h3.shape
    rows = nb * REC_T
    return pl.pallas_call(
        _rec_kernel,
        out_shape=jax.ShapeDtypeStruct((nb, seq, D_MODEL), BF16),
        grid=(seq // REC_T,),
        in_specs=[pl.BlockSpec((nb, REC_T, D_MODEL), lambda i: (0, i, 0)),
                  _const_spec((1, D_MODEL)),
                  _const_spec((D_MODEL, 2 * D_MODEL)),
                  _const_spec((1, 2 * D_MODEL)),
                  _const_spec((CONV_WIDTH, D_MODEL)),
                  _const_spec((1, D_MODEL)),
                  _const_spec((N_RG_BLOCKS, RG_BLOCK, 2 * RG_BLOCK)),
                  _const_spec((N_RG_BLOCKS, 1, 2 * RG_BLOCK)),
                  _const_spec((1, D_MODEL))],
        out_specs=pl.BlockSpec((nb, REC_T, D_MODEL), lambda i: (0, i, 0)),
        scratch_shapes=[pltpu.VMEM((nb, SUBLANES + REC_T, D_MODEL), F32),
                        pltpu.VMEM((N_RG_BLOCKS, rows, RG_BLOCK), F32),
                        pltpu.VMEM((N_RG_BLOCKS, rows, RG_BLOCK), F32),
                        pltpu.VMEM((N_RG_BLOCKS, rows, RG_BLOCK), F32),
                        pltpu.VMEM((N_RG_BLOCKS, nb, RG_BLOCK), F32)],
        compiler_params=_params(("arbitrary",)),
        name="rec_front",
    )(h3, g, w_in, b_in, conv_w, conv_b, w_gate, b_gate, a_param)


def _proj_mlp_kernel(h_ref, m_ref, wo_ref, bo_ref, g_ref, w1_ref, w2_ref, gf_ref, o_ref,
                     *, final_norm):
    h1 = h_ref[...] + jnp.dot(m_ref[...], wo_ref[...], preferred_element_type=F32) + bo_ref[...]
    hn = _rms(h1, g_ref[...]).astype(BF16)
    acc = h1
    for f in range(D_FF // FF_TILE):
        fs = slice(f * FF_TILE, (f + 1) * FF_TILE)
        z = jnp.maximum(jnp.dot(hn, w1_ref[:, fs], preferred_element_type=F32), 0.0)
        acc = acc + jnp.dot((z * z).astype(BF16), w2_ref[fs, :], preferred_element_type=F32)
    if final_norm:
        acc = _rms(acc, gf_ref[...])
    o_ref[...] = acc


def _proj_mlp(h, mix, w_o, b_o, g, w1, w2, g_final, final_norm):
    t = h.shape[0]
    row = lambda i: (i, 0)
    single = pl.Buffered(1)
    return pl.pallas_call(
        functools.partial(_proj_mlp_kernel, final_norm=final_norm),
        out_shape=jax.ShapeDtypeStruct((t, D_MODEL), F32),
        grid=(t // ROW_TILE,),
        in_specs=[pl.BlockSpec((ROW_TILE, D_MODEL), row),
                  pl.BlockSpec((ROW_TILE, D_MODEL), row),
                  pl.BlockSpec((D_MODEL, D_MODEL), lambda i: (0, 0), pipeline_mode=single),
                  _const_spec((1, D_MODEL)),
                  _const_spec((1, D_MODEL)),
                  pl.BlockSpec((D_MODEL, D_FF), lambda i: (0, 0), pipeline_mode=single),
                  pl.BlockSpec((D_FF, D_MODEL), lambda i: (0, 0), pipeline_mode=single),
                  _const_spec((1, D_MODEL))],
        out_specs=pl.BlockSpec((ROW_TILE, D_MODEL), row),
        compiler_params=_params(("parallel",)),
        name="proj_mlp",
    )(h, mix, w_o, b_o, g, w1, w2, g_final)


def kernel(x, norm_mix, norm_mlp, attn_w_qkv, attn_w_o, attn_rel_bias, rec_w_in, rec_b_in,
           rec_conv_w, rec_conv_b, rec_w_ga, rec_b_ga, rec_w_gx, rec_b_gx, rec_a_param,
           rec_w_o, rec_b_o, mlp_w1, mlp_w2, norm_final):
    batch, seq, _ = x.shape
    depth = norm_mix.shape[0]
    h = x.reshape(batch * seq, D_MODEL)
    row = lambda v: v.reshape(1, -1).astype(F32)
    zero_bias = jnp.zeros((1, D_MODEL), F32)
    for layer in range(depth):
        j = layer // 2
        g_mix = row(norm_mix[layer])
        if layer % 2 == 0:
            qkv = _qkv(h, g_mix, attn_w_qkv[j].astype(BF16))
            mix = _attn_core(qkv, _rel_bias(attn_rel_bias[j]), batch, seq)
            w_o, b_o = attn_w_o[j], zero_bias
        else:
            w_gate = jnp.concatenate([rec_w_ga[j], rec_w_gx[j]], axis=-1).astype(BF16)
            b_gate = jnp.concatenate([rec_b_ga[j], rec_b_gx[j]], axis=-1)[:, None, :]
            mix = _rec_front(h.reshape(batch, seq, D_MODEL), g_mix,
                             rec_w_in[j].astype(BF16), row(rec_b_in[j]),
                             rec_conv_w[j], row(rec_conv_b[j]), w_gate, b_gate,
                             row(rec_a_param[j]))
            mix = mix.reshape(batch * seq, D_MODEL)
            w_o, b_o = rec_w_o[j], row(rec_b_o[j])
        h = _proj_mlp(h, mix, w_o.astype(BF16), b_o, row(norm_mlp[layer]),
                      mlp_w1[layer].astype(BF16), mlp_w2[layer].astype(BF16),
                      row(norm_final), final_norm=(layer == depth - 1))
    return h.reshape(batch, seq, D_MODEL)
```

```python
import functools
import math

import numpy as np
import jax
import jax.numpy as jnp
from jax import lax
from jax.experimental import pallas as pl
from jax.experimental.pallas import tpu as pltpu

D_MODEL = 1024
N_HEADS = 16
HEAD_DIM = D_MODEL // N_HEADS
CHUNK = 64
LEFT_CHUNKS = 8
BAND = (LEFT_CHUNKS + 1) * CHUNK
PAD = LEFT_CHUNKS * CHUNK
MAX_REL_DIST = 128
NEG_INF = -1e30
N_RG_BLOCKS = 8
RG_BLOCK = D_MODEL // N_RG_BLOCKS
CONV_WIDTH = 4
RG_C = 8.0
D_FF = 4 * D_MODEL
RMS_EPS = 1e-6
LOG2_E = math.log2(math.e)

LANES = 128
SUBLANES = 8
HEADS_PER_VREG = LANES // HEAD_DIM
N_HEAD_PAIRS = N_HEADS // HEADS_PER_VREG
VMEM_LIMIT = 56 * 1024 * 1024

ROW_TILE = 512
CAST_TILE_ELEMS = 1024 * 1024
CHUNKS_PER_STEP = 16
REC_T = 64
FF_TILE = 1024

BF16 = jnp.bfloat16
F32 = jnp.float32


def _rms(x, g):
    ms = jnp.mean(x * x, axis=-1, keepdims=True)
    return x * lax.rsqrt(ms + RMS_EPS) * g


def _const_spec(shape):
    zeros = (0,) * len(shape)
    return pl.BlockSpec(shape, lambda *_: zeros)


def _params(sem):
    return pltpu.CompilerParams(dimension_semantics=sem, vmem_limit_bytes=VMEM_LIMIT)


def _cast_kernel(x_ref, o_ref):
    o_ref[...] = x_ref[...].astype(o_ref.dtype)


def _to_bf16(w):
    cols = w.shape[-1]
    flat = w.reshape(-1, cols)
    rows = flat.shape[0]
    tile = 1 << ((CAST_TILE_ELEMS // cols).bit_length() - 1)
    assert rows % tile == 0
    out = pl.pallas_call(
        _cast_kernel,
        out_shape=jax.ShapeDtypeStruct((rows, cols), BF16),
        grid=(rows // tile,),
        in_specs=[pl.BlockSpec((tile, cols), lambda i: (i, 0))],
        out_specs=pl.BlockSpec((tile, cols), lambda i: (i, 0)),
        compiler_params=_params(("parallel",)),
        name="cast_bf16",
    )(flat)
    return out.reshape(w.shape)


def _layer_spec(shape, layer):
    return pl.BlockSpec((None,) + tuple(shape), lambda *_: (layer, 0, 0),
                        pipeline_mode=pl.Buffered(1))


def _qkv_kernel(h_ref, g_ref, w_ref, q_ref, k_ref, v_ref):
    hn = _rms(h_ref[...], g_ref[...]).astype(BF16)
    q = jnp.dot(hn, w_ref[:, 0:D_MODEL], preferred_element_type=F32)
    q_ref[...] = (q * (HEAD_DIM ** -0.5 * LOG2_E)).astype(BF16)
    k_ref[...] = jnp.dot(hn, w_ref[:, D_MODEL:2 * D_MODEL],
                         preferred_element_type=F32).astype(BF16)
    v_ref[...] = jnp.dot(hn, w_ref[:, 2 * D_MODEL:3 * D_MODEL],
                         preferred_element_type=F32).astype(BF16)


def _qkv(h, g, w, layer):
    t = h.shape[0]
    row = lambda i: (i, 0)
    out = jax.ShapeDtypeStruct((t, D_MODEL), BF16)
    return pl.pallas_call(
        _qkv_kernel,
        out_shape=(out, out, out),
        grid=(t // ROW_TILE,),
        in_specs=[pl.BlockSpec((ROW_TILE, D_MODEL), row),
                  _const_spec((1, D_MODEL)),
                  _layer_spec((D_MODEL, 3 * D_MODEL), layer)],
        out_specs=(pl.BlockSpec((ROW_TILE, D_MODEL), row),) * 3,
        compiler_params=_params(("parallel",)),
        name="attn_qkv",
    )(h, g, w)


def _attn_kernel(q_ref, k_ref, v_ref, bias_ref, o_ref, kpad, vpad, s_scr, e_scr, l_scr):
    g = pl.program_id(1)
    c0 = g * CHUNKS_PER_STEP

    @pl.when(g == 0)
    def _():
        zeros = jnp.zeros((PAD, D_MODEL), BF16)
        kpad[0:PAD, :] = zeros
        vpad[0:PAD, :] = zeros
        kpad[PAD:, :] = k_ref[...]
        vpad[PAD:, :] = v_ref[...]

    col = lax.broadcasted_iota(jnp.int32, (1, BAND), 1)
    first_head = lax.broadcasted_iota(jnp.int32, (1, LANES), 1) < HEAD_DIM
    mask_a = first_head.astype(F32).astype(BF16)
    mask_b = (1.0 - first_head.astype(F32)).astype(BF16)

    def lanes(hp):
        return slice(hp * LANES, (hp + 1) * LANES)

    def scores(c, qrow, hp, masked):
        q2 = q_ref[pl.ds(qrow, CHUNK), lanes(hp)]
        kb = kpad[pl.ds(pl.multiple_of(c * CHUNK, CHUNK), BAND), lanes(hp)]
        qs = jnp.concatenate([q2 * mask_a, q2 * mask_b], axis=0)
        s = lax.dot_general(qs, kb, (((1,), (1,)), ((), ())), preferred_element_type=F32)
        s = s + bias_ref[hp]
        if masked:
            s = jnp.where(col >= (LEFT_CHUNKS - c) * CHUNK, s, NEG_INF)
        return s

    def softmax_terms(s):
        m = jnp.max(s, axis=-1, keepdims=True)
        e = jnp.exp2(s - m)
        return e.astype(BF16), jnp.sum(e, axis=-1, keepdims=True)

    def weighted_values(c, qrow, hp, e, l):
        vb = vpad[pl.ds(pl.multiple_of(c * CHUNK, CHUNK), BAND), lanes(hp)]
        o2 = jnp.dot(e, vb, preferred_element_type=F32) / l
        o = jnp.where(first_head, o2[:CHUNK], o2[CHUNK:])
        o_ref[pl.ds(qrow, CHUNK), lanes(hp)] = o.astype(BF16)

    def chunk(c, qrow, c_next, qrow_next, deferred, masked):
        if deferred is not None:
            weighted_values(deferred[0], deferred[1], N_HEAD_PAIRS - 1, e_scr[...], l_scr[...])
        s = s_scr[...]
        for hp in range(N_HEAD_PAIRS):
            last = hp == N_HEAD_PAIRS - 1
            if last:
                s_scr[...] = scores(c_next, qrow_next, 0, masked)
            else:
                s_next = scores(c, qrow, hp + 1, masked)
            e, l = softmax_terms(s)
            if last:
                e_scr[...] = e
                l_scr[...] = l
            else:
                weighted_values(c, qrow, hp, e, l)
                s = s_next

    def body(masked, cc, carry):
        qrow = pl.multiple_of(cc * CHUNK, CHUNK)
        nxt = jnp.minimum(cc + 1, CHUNKS_PER_STEP - 1)
        chunk(c0 + cc, qrow, c0 + nxt, pl.multiple_of(nxt * CHUNK, CHUNK),
              (c0 + cc - 1, pl.multiple_of((cc - 1) * CHUNK, CHUNK)), masked)
        return carry

    n_masked = jnp.clip(LEFT_CHUNKS - c0, 1, CHUNKS_PER_STEP)
    s_scr[...] = scores(c0, 0, 0, True)
    chunk(c0, 0, c0 + 1, CHUNK, None, True)
    lax.fori_loop(1, n_masked, functools.partial(body, True), 0)
    lax.fori_loop(n_masked, CHUNKS_PER_STEP, functools.partial(body, False), 0)
    weighted_values(c0 + CHUNKS_PER_STEP - 1, (CHUNKS_PER_STEP - 1) * CHUNK,
                    N_HEAD_PAIRS - 1, e_scr[...], l_scr[...])


def _attn_core(q, k, v, bias2, batch, seq):
    rows = CHUNKS_PER_STEP * CHUNK
    steps = seq // rows
    tok = lambda b, g: (b * steps + g, 0)
    whole_batch = lambda b, g: (b, 0)
    return pl.pallas_call(
        _attn_kernel,
        out_shape=jax.ShapeDtypeStruct((batch * seq, D_MODEL), BF16),
        grid=(batch, steps),
        in_specs=[pl.BlockSpec((rows, D_MODEL), tok),
                  pl.BlockSpec((seq, D_MODEL), whole_batch),
                  pl.BlockSpec((seq, D_MODEL), whole_batch),
                  _const_spec((N_HEAD_PAIRS, HEADS_PER_VREG * CHUNK, BAND))],
        out_specs=pl.BlockSpec((rows, D_MODEL), tok),
        scratch_shapes=[pltpu.VMEM((PAD + seq, D_MODEL), BF16),
                        pltpu.VMEM((PAD + seq, D_MODEL), BF16),
                        pltpu.VMEM((HEADS_PER_VREG * CHUNK, BAND), F32),
                        pltpu.VMEM((HEADS_PER_VREG * CHUNK, BAND), BF16),
                        pltpu.VMEM((HEADS_PER_VREG * CHUNK, 1), F32)],
        compiler_params=_params(("parallel", "arbitrary")),
        name="attn_core",
    )(q, k, v, bias2)


def _rel_bias(rel_table):
    period = BAND + CHUNK
    t = np.arange(period)
    rel = PAD - np.where(t <= BAND, t, t - period)
    idx = np.clip(rel, -MAX_REL_DIST, MAX_REL_DIST) + MAX_REL_DIST
    row = rel_table[:, idx].astype(F32) * LOG2_E
    bias = jnp.tile(row, (1, CHUNK))[:, :CHUNK * (period - 1)]
    bias = bias.reshape(N_HEADS, CHUNK, period - 1)[:, :, :BAND]
    return bias.reshape(N_HEAD_PAIRS, HEADS_PER_VREG * CHUNK, BAND)


REC_HALO = (CONV_WIDTH - 1) * SUBLANES
TIME_BLOCKS = REC_T // SUBLANES


def _rec_kernel(h_ref, g_ref, win_ref, bin_ref, cw_ref, cb_ref, wg_ref, bg_ref,
                ap_ref, o_ref, xs, a_s, b_s, hs_s, y_s, carry):
    nb = h_ref.shape[0]
    rows = nb * REC_T
    blk = nb * SUBLANES
    i = pl.program_id(0)

    @pl.when(i == 0)
    def _():
        xs[0:REC_HALO, :] = jnp.zeros((REC_HALO, D_MODEL), F32)
        carry[...] = jnp.zeros_like(carry)

    r = lax.broadcasted_iota(jnp.int32, (blk, blk), 0)
    c = lax.broadcasted_iota(jnp.int32, (blk, blk), 1)
    hi, lo = SUBLANES.bit_length() - 1, SUBLANES - 1
    swap = jnp.where(((r >> hi) == (c & lo)) & ((r & lo) == (c >> hi)), 1.0, 0.0).astype(BF16)

    x = h_ref[...].reshape(rows, D_MODEL)
    xn = _rms(x, g_ref[...])
    parts = []
    for j in range(TIME_BLOCKS):
        group = jnp.concatenate(
            [xn[b * REC_T + j * SUBLANES:b * REC_T + (j + 1) * SUBLANES] for b in range(nb)],
            axis=0).astype(BF16)
        parts.append(jnp.dot(swap, group, preferred_element_type=F32).astype(BF16))
    hn = jnp.concatenate(parts, axis=0)

    u = jnp.dot(hn, win_ref[...], preferred_element_type=F32) + bin_ref[...]
    y_s[...] = jax.nn.gelu(u[:, D_MODEL:], approximate=True)

    xs[REC_HALO:, :] = u[:, :D_MODEL]
    xc = cb_ref[...] + xs[0:rows, :] * cw_ref[0:1, :]
    for tap in range(1, CONV_WIDTH):
        xc = xc + xs[tap * nb:tap * nb + rows, :] * cw_ref[tap:tap + 1, :]
    xs[0:REC_HALO, :] = xs[rows:rows + REC_HALO, :]
    xcb = xc.astype(BF16)

    ap = ap_ref[...]
    neg_sp = -RG_C * (jnp.maximum(-ap, 0.0) + jnp.log1p(jnp.exp(-jnp.abs(ap))))
    for n in range(N_RG_BLOCKS):
        cs = slice(n * RG_BLOCK, (n + 1) * RG_BLOCK)
        gz = jnp.dot(xcb[:, cs], wg_ref[n], preferred_element_type=F32) + bg_ref[n]
        rg = jax.nn.sigmoid(gz[:, :RG_BLOCK])
        ig = jax.nn.sigmoid(gz[:, RG_BLOCK:])
        a = jnp.exp(rg * neg_sp[:, cs])
        om = 1.0 - a * a
        mult = om * lax.rsqrt(jnp.maximum(om, 1e-30))
        a_s[:, cs] = a
        b_s[:, cs] = mult * (ig * xc[:, cs])

    def step(t, hc):
        rows_t = pl.ds(pl.multiple_of(t * nb, nb), nb)
        hc = a_s[rows_t, :] * hc + b_s[rows_t, :]
        hs_s[rows_t, :] = hc
        return hc

    carry[...] = lax.fori_loop(0, REC_T, step, carry[...], unroll=8)

    back = []
    for j in range(TIME_BLOCKS):
        rs = slice(j * blk, (j + 1) * blk)
        out = (hs_s[rs, :] * y_s[rs, :]).astype(BF16)
        back.append(jnp.dot(swap, out, preferred_element_type=F32))
    for b in range(nb):
        o_ref[b] = jnp.concatenate(
            [back[j][b * SUBLANES:(b + 1) * SUBLANES] for j in range(TIME_BLOCKS)],
            axis=0).astype(BF16)


def _rec_front(h3, g, w_in, layer, b_in, conv_w, conv_b, w_gate, b_gate, a_param):
    nb, seq, _ = h3.shape
    assert nb == SUBLANES
    rows = nb * REC_T
    return pl.pallas_call(
        _rec_kernel,
        out_shape=jax.ShapeDtypeStruct((nb, seq, D_MODEL), BF16),
        grid=(seq // REC_T,),
        in_specs=[pl.BlockSpec((nb, REC_T, D_MODEL), lambda i: (0, i, 0)),
                  _const_spec((1, D_MODEL)),
                  _layer_spec((D_MODEL, 2 * D_MODEL), layer),
                  _const_spec((1, 2 * D_MODEL)),
                  _const_spec((CONV_WIDTH, D_MODEL)),
                  _const_spec((1, D_MODEL)),
                  _const_spec((N_RG_BLOCKS, RG_BLOCK, 2 * RG_BLOCK)),
                  _const_spec((N_RG_BLOCKS, 1, 2 * RG_BLOCK)),
                  _const_spec((1, D_MODEL))],
        out_specs=pl.BlockSpec((nb, REC_T, D_MODEL), lambda i: (0, i, 0)),
        scratch_shapes=[pltpu.VMEM((REC_HALO + rows, D_MODEL), F32),
                        pltpu.VMEM((rows, D_MODEL), F32),
                        pltpu.VMEM((rows, D_MODEL), F32),
                        pltpu.VMEM((rows, D_MODEL), F32),
                        pltpu.VMEM((rows, D_MODEL), F32),
                        pltpu.VMEM((nb, D_MODEL), F32)],
        compiler_params=_params(("arbitrary",)),
        name="rec_front",
    )(h3, g, w_in, b_in, conv_w, conv_b, w_gate, b_gate, a_param)


def _proj_mlp_kernel(h_ref, m_ref, wo_ref, bo_ref, g_ref, w1_ref, w2_ref, gf_ref, o_ref,
                     *, final_norm):
    h1 = h_ref[...] + jnp.dot(m_ref[...], wo_ref[...], preferred_element_type=F32) + bo_ref[...]
    hn = _rms(h1, g_ref[...]).astype(BF16)
    acc = h1
    for f in range(D_FF // FF_TILE):
        fs = slice(f * FF_TILE, (f + 1) * FF_TILE)
        z = jnp.maximum(jnp.dot(hn, w1_ref[:, fs], preferred_element_type=F32), 0.0)
        acc = acc + jnp.dot((z * z).astype(BF16), w2_ref[fs, :], preferred_element_type=F32)
    if final_norm:
        acc = _rms(acc, gf_ref[...])
    o_ref[...] = acc


def _proj_mlp(h, mix, w_o, mix_layer, b_o, g, w1, w2, layer, g_final, final_norm):
    t = h.shape[0]
    row = lambda i: (i, 0)
    return pl.pallas_call(
        functools.partial(_proj_mlp_kernel, final_norm=final_norm),
        out_shape=jax.ShapeDtypeStruct((t, D_MODEL), F32),
        grid=(t // ROW_TILE,),
        in_specs=[pl.BlockSpec((ROW_TILE, D_MODEL), row),
                  pl.BlockSpec((ROW_TILE, D_MODEL), row),
                  _layer_spec((D_MODEL, D_MODEL), mix_layer),
                  _const_spec((1, D_MODEL)),
                  _const_spec((1, D_MODEL)),
                  _layer_spec((D_MODEL, D_FF), layer),
                  _layer_spec((D_FF, D_MODEL), layer),
                  _const_spec((1, D_MODEL))],
        out_specs=pl.BlockSpec((ROW_TILE, D_MODEL), row),
        compiler_params=_params(("parallel",)),
        name="proj_mlp",
    )(h, mix, w_o, b_o, g, w1, w2, g_final)


def kernel(x, norm_mix, norm_mlp, attn_w_qkv, attn_w_o, attn_rel_bias, rec_w_in, rec_b_in,
           rec_conv_w, rec_conv_b, rec_w_ga, rec_b_ga, rec_w_gx, rec_b_gx, rec_a_param,
           rec_w_o, rec_b_o, mlp_w1, mlp_w2, norm_final):
    batch, seq, _ = x.shape
    depth = norm_mix.shape[0]
    h = x.reshape(batch * seq, D_MODEL)
    row = lambda v: v.reshape(1, -1).astype(F32)
    zero_bias = jnp.zeros((1, D_MODEL), F32)
    w_qkv, w_attn_o = _to_bf16(attn_w_qkv), _to_bf16(attn_w_o)
    w_in, w_rec_o = _to_bf16(rec_w_in), _to_bf16(rec_w_o)
    w1, w2 = _to_bf16(mlp_w1), _to_bf16(mlp_w2)
    for layer in range(depth):
        j = layer // 2
        g_mix = row(norm_mix[layer])
        if layer % 2 == 0:
            q, k, v = _qkv(h, g_mix, w_qkv, j)
            mix = _attn_core(q, k, v, _rel_bias(attn_rel_bias[j]), batch, seq)
            w_o, b_o = w_attn_o, zero_bias
        else:
            w_gate = jnp.concatenate([rec_w_ga[j], rec_w_gx[j]], axis=-1).astype(BF16)
            b_gate = jnp.concatenate([rec_b_ga[j], rec_b_gx[j]], axis=-1)[:, None, :]
            mix = _rec_front(h.reshape(batch, seq, D_MODEL), g_mix, w_in, j,
                             row(rec_b_in[j]), rec_conv_w[j], row(rec_conv_b[j]),
                             w_gate, b_gate, row(rec_a_param[j]))
            mix = mix.reshape(batch * seq, D_MODEL)
            w_o, b_o = w_rec_o, row(rec_b_o[j])
        h = _proj_mlp(h, mix, w_o, j, b_o, row(norm_mlp[layer]), w1, w2, layer,
                      row(norm_final), final_norm=(layer == depth - 1))
    return h.reshape(batch, seq, D_MODEL)
```

```python
import functools
import math

import numpy as np
import jax
import jax.numpy as jnp
from jax import lax
from jax.experimental import pallas as pl
from jax.experimental.pallas import tpu as pltpu

D_MODEL = 1024
N_HEADS = 16
HEAD_DIM = D_MODEL // N_HEADS
CHUNK = 64
LEFT_CHUNKS = 8
BAND = (LEFT_CHUNKS + 1) * CHUNK
PAD = LEFT_CHUNKS * CHUNK
MAX_REL_DIST = 128
NEG_INF = -1e30
FAR_COLS = (PAD - MAX_REL_DIST) // 128 * 128
N_RG_BLOCKS = 8
RG_BLOCK = D_MODEL // N_RG_BLOCKS
CONV_WIDTH = 4
RG_C = 8.0
D_FF = 4 * D_MODEL
RMS_EPS = 1e-6
LOG2_E = math.log2(math.e)

LANES = 128
SUBLANES = 8
HEADS_PER_VREG = LANES // HEAD_DIM
N_HEAD_PAIRS = N_HEADS // HEADS_PER_VREG
VMEM_LIMIT = 56 * 1024 * 1024

ROW_TILE = 512
CAST_TILE_ELEMS = 1024 * 1024
CHUNKS_PER_STEP = 16
CHUNKS_PER_BODY = 2
SCORE_LOOKAHEAD = 3
REC_T = 64
FF_TILE = 1024

BF16 = jnp.bfloat16
F32 = jnp.float32


def _rms(x, g):
    ms = jnp.mean(x * x, axis=-1, keepdims=True)
    return x * lax.rsqrt(ms + RMS_EPS) * g


def _const_spec(shape):
    zeros = (0,) * len(shape)
    return pl.BlockSpec(shape, lambda *_: zeros)


def _params(sem):
    return pltpu.CompilerParams(dimension_semantics=sem, vmem_limit_bytes=VMEM_LIMIT)


def _cast_kernel(x_ref, o_ref):
    o_ref[...] = x_ref[...].astype(o_ref.dtype)


def _to_bf16(w):
    cols = w.shape[-1]
    flat = w.reshape(-1, cols)
    rows = flat.shape[0]
    tile = 1 << ((CAST_TILE_ELEMS // cols).bit_length() - 1)
    assert rows % tile == 0
    out = pl.pallas_call(
        _cast_kernel,
        out_shape=jax.ShapeDtypeStruct((rows, cols), BF16),
        grid=(rows // tile,),
        in_specs=[pl.BlockSpec((tile, cols), lambda i: (i, 0))],
        out_specs=pl.BlockSpec((tile, cols), lambda i: (i, 0)),
        compiler_params=_params(("parallel",)),
        name="cast_bf16",
    )(flat)
    return out.reshape(w.shape)


def _layer_spec(shape, layer):
    return pl.BlockSpec((None,) + tuple(shape), lambda *_: (layer, 0, 0),
                        pipeline_mode=pl.Buffered(1))


def _qkv_kernel(h_ref, g_ref, w_ref, q_ref, k_ref, v_ref):
    hn = _rms(h_ref[...], g_ref[...]).astype(BF16)
    q = jnp.dot(hn, w_ref[:, 0:D_MODEL], preferred_element_type=F32)
    q_ref[...] = (q * (HEAD_DIM ** -0.5 * LOG2_E)).astype(BF16)
    k_ref[...] = jnp.dot(hn, w_ref[:, D_MODEL:2 * D_MODEL],
                         preferred_element_type=F32).astype(BF16)
    v_ref[...] = jnp.dot(hn, w_ref[:, 2 * D_MODEL:3 * D_MODEL],
                         preferred_element_type=F32).astype(BF16)


def _qkv(h, g, w, layer):
    t = h.shape[0]
    row = lambda i: (i, 0)
    out = jax.ShapeDtypeStruct((t, D_MODEL), BF16)
    return pl.pallas_call(
        _qkv_kernel,
        out_shape=(out, out, out),
        grid=(t // ROW_TILE,),
        in_specs=[pl.BlockSpec((ROW_TILE, D_MODEL), row),
                  _const_spec((1, D_MODEL)),
                  _layer_spec((D_MODEL, 3 * D_MODEL), layer)],
        out_specs=(pl.BlockSpec((ROW_TILE, D_MODEL), row),) * 3,
        compiler_params=_params(("parallel",)),
        name="attn_qkv",
    )(h, g, w)


def _attn_kernel(q_ref, k_ref, v_ref, bias_ref, o_ref, kpad, vpad, s_scr, e_scr, l_scr):
    g = pl.program_id(1)
    c0 = g * CHUNKS_PER_STEP

    @pl.when(g == 0)
    def _():
        zeros = jnp.zeros((PAD, D_MODEL), BF16)
        kpad[0:PAD, :] = zeros
        vpad[0:PAD, :] = zeros
        kpad[PAD:, :] = k_ref[...]
        vpad[PAD:, :] = v_ref[...]

    col = lax.broadcasted_iota(jnp.int32, (1, BAND), 1)
    first_head = lax.broadcasted_iota(jnp.int32, (1, LANES), 1) < HEAD_DIM
    mask_a = first_head.astype(F32).astype(BF16)
    mask_b = (1.0 - first_head.astype(F32)).astype(BF16)

    def lanes(hp):
        return slice(hp * LANES, (hp + 1) * LANES)

    def scores(c, qrow, hp, masked):
        q2 = q_ref[pl.ds(qrow, CHUNK), lanes(hp)]
        kb = kpad[pl.ds(pl.multiple_of(c * CHUNK, CHUNK), BAND), lanes(hp)]
        qs = jnp.concatenate([q2 * mask_a, q2 * mask_b], axis=0)
        s = lax.dot_general(qs, kb, (((1,), (1,)), ((), ())), preferred_element_type=F32)
        s = jnp.concatenate([s[:, :FAR_COLS], s[:, FAR_COLS:] + bias_ref[hp]], axis=1)
        if masked:
            s = jnp.where(col >= (LEFT_CHUNKS - c) * CHUNK, s, NEG_INF)
        return s

    def softmax_terms(s):
        m = jnp.max(s, axis=-1, keepdims=True)
        e = jnp.exp2(s - m)
        return e.astype(BF16), jnp.sum(e, axis=-1, keepdims=True)

    def weighted_values(c, qrow, hp, e, l):
        vb = vpad[pl.ds(pl.multiple_of(c * CHUNK, CHUNK), BAND), lanes(hp)]
        o2 = jnp.dot(e, vb, preferred_element_type=F32) / l
        o = jnp.where(first_head, o2[:CHUNK], o2[CHUNK:])
        o_ref[pl.ds(qrow, CHUNK), lanes(hp)] = o.astype(BF16)

    def stages(first, masked, deferred):
        def at(idx):
            idx = idx if isinstance(idx, int) else jnp.minimum(idx, CHUNKS_PER_STEP - 1)
            row = idx * CHUNK
            return c0 + idx, row if isinstance(row, int) else pl.multiple_of(row, CHUNK)

        todo = [at(first + i) + (hp,) for i in range(CHUNKS_PER_BODY)
                for hp in range(N_HEAD_PAIRS)]
        after = at(first + CHUNKS_PER_BODY)
        if deferred:
            weighted_values(*at(first - 1), N_HEAD_PAIRS - 1, e_scr[...], l_scr[...])
        ahead = {}
        for i, (c, qrow, hp) in enumerate(todo):
            j = i + SCORE_LOOKAHEAD
            if j < len(todo):
                ahead[j] = scores(*todo[j], masked)
            s = s_scr[i] if i < SCORE_LOOKAHEAD else ahead.pop(i)
            if j >= len(todo):
                s_scr[j - len(todo)] = scores(*after, j - len(todo), masked)
            e, l = softmax_terms(s)
            if i == len(todo) - 1:
                e_scr[...] = e
                l_scr[...] = l
            else:
                weighted_values(c, qrow, hp, e, l)

    def body(masked, bb, carry):
        stages(bb * CHUNKS_PER_BODY, masked, True)
        return carry

    n_bodies = CHUNKS_PER_STEP // CHUNKS_PER_BODY
    n_masked = jnp.clip(pl.cdiv(LEFT_CHUNKS - c0, CHUNKS_PER_BODY), 1, n_bodies)
    for hp in range(SCORE_LOOKAHEAD):
        s_scr[hp] = scores(c0, 0, hp, True)
    stages(0, True, False)
    lax.fori_loop(1, n_masked, functools.partial(body, True), 0)
    lax.fori_loop(n_masked, n_bodies, functools.partial(body, False), 0)
    weighted_values(c0 + CHUNKS_PER_STEP - 1, (CHUNKS_PER_STEP - 1) * CHUNK,
                    N_HEAD_PAIRS - 1, e_scr[...], l_scr[...])


def _attn_core(q, k, v, bias2, batch, seq):
    rows = CHUNKS_PER_STEP * CHUNK
    steps = seq // rows
    tok = lambda b, g: (b * steps + g, 0)
    whole_batch = lambda b, g: (b, 0)
    return pl.pallas_call(
        _attn_kernel,
        out_shape=jax.ShapeDtypeStruct((batch * seq, D_MODEL), BF16),
        grid=(batch, steps),
        in_specs=[pl.BlockSpec((rows, D_MODEL), tok),
                  pl.BlockSpec((seq, D_MODEL), whole_batch),
                  pl.BlockSpec((seq, D_MODEL), whole_batch),
                  _const_spec((N_HEAD_PAIRS, HEADS_PER_VREG * CHUNK, BAND - FAR_COLS))],
        out_specs=pl.BlockSpec((rows, D_MODEL), tok),
        scratch_shapes=[pltpu.VMEM((PAD + seq, D_MODEL), BF16),
                        pltpu.VMEM((PAD + seq, D_MODEL), BF16),
                        pltpu.VMEM((SCORE_LOOKAHEAD, HEADS_PER_VREG * CHUNK, BAND), F32),
                        pltpu.VMEM((HEADS_PER_VREG * CHUNK, BAND), BF16),
                        pltpu.VMEM((HEADS_PER_VREG * CHUNK, 1), F32)],
        compiler_params=_params(("parallel", "arbitrary")),
        name="attn_core",
    )(q, k, v, bias2)


def _rel_bias(rel_table):
    period = BAND + CHUNK
    t = np.arange(period)
    rel = PAD - np.where(t <= BAND, t, t - period)
    idx = np.clip(rel, -MAX_REL_DIST, MAX_REL_DIST) + MAX_REL_DIST
    row = rel_table[:, idx].astype(F32)
    row = (row - row[:, 0:1]) * LOG2_E
    bias = jnp.tile(row, (1, CHUNK))[:, :CHUNK * (period - 1)]
    bias = bias.reshape(N_HEADS, CHUNK, period - 1)[:, :, FAR_COLS:BAND]
    return bias.reshape(N_HEAD_PAIRS, HEADS_PER_VREG * CHUNK, BAND - FAR_COLS)


REC_HALO = (CONV_WIDTH - 1) * SUBLANES
TIME_BLOCKS = REC_T // SUBLANES
REC_PARTS = 2


def _rec_kernel(h_ref, g_ref, win_ref, bin_ref, cw_ref, cb_ref, wg_ref, bg_ref,
                ap_ref, o_ref, xs, a_s, b_s, hs_s, y_s, carry):
    nb = h_ref.shape[0]
    rows = nb * REC_T
    blk = nb * SUBLANES
    i = pl.program_id(0)

    @pl.when(i == 0)
    def _():
        xs[0:REC_HALO, :] = jnp.zeros((REC_HALO, D_MODEL), F32)
        carry[...] = jnp.zeros_like(carry)

    r = lax.broadcasted_iota(jnp.int32, (blk, blk), 0)
    c = lax.broadcasted_iota(jnp.int32, (blk, blk), 1)
    hi, lo = SUBLANES.bit_length() - 1, SUBLANES - 1
    swap = jnp.where(((r >> hi) == (c & lo)) & ((r & lo) == (c >> hi)), 1.0, 0.0).astype(BF16)

    x = h_ref[...].reshape(rows, D_MODEL)
    ap = ap_ref[...]
    neg_sp = -RG_C * (jnp.maximum(-ap, 0.0) + jnp.log1p(jnp.exp(-jnp.abs(ap))))
    half_sp2 = (0.5 * LOG2_E) * neg_sp
    part_rows = rows // REC_PARTS
    part_blocks = TIME_BLOCKS // REC_PARTS

    def project(p):
        parts = []
        for j in range(p * part_blocks, (p + 1) * part_blocks):
            group = jnp.concatenate(
                [x[b * REC_T + j * SUBLANES:b * REC_T + (j + 1) * SUBLANES] for b in range(nb)],
                axis=0)
            group = _rms(group, g_ref[...]).astype(BF16)
            parts.append(jnp.dot(swap, group, preferred_element_type=F32).astype(BF16))
        hn = jnp.concatenate(parts, axis=0)
        u = jnp.dot(hn, win_ref[...], preferred_element_type=F32) + bin_ref[...]
        rs = slice(p * part_rows, (p + 1) * part_rows)
        xs[REC_HALO + p * part_rows:REC_HALO + (p + 1) * part_rows, :] = u[:, :D_MODEL]
        y_s[rs, :] = u[:, D_MODEL:]

    def gate_inputs(p):
        r0 = p * part_rows
        rs = slice(r0, r0 + part_rows)
        y_s[rs, :] = jax.nn.gelu(y_s[rs, :], approximate=True)
        xc = cb_ref[...] + xs[r0:r0 + part_rows, :] * cw_ref[0:1, :]
        for tap in range(1, CONV_WIDTH):
            xc = xc + xs[r0 + tap * nb:r0 + tap * nb + part_rows, :] * cw_ref[tap:tap + 1, :]
        xcb = xc.astype(BF16)
        gz = [jnp.dot(xcb[:, n * RG_BLOCK:(n + 1) * RG_BLOCK], wg_ref[n],
                      preferred_element_type=F32) + bg_ref[n] for n in range(N_RG_BLOCKS)]
        return xc, gz

    def recurrence_inputs(p, xc, gz):
        rs = slice(p * part_rows, (p + 1) * part_rows)
        for n in range(N_RG_BLOCKS):
            cs = slice(n * RG_BLOCK, (n + 1) * RG_BLOCK)
            t = jnp.tanh(gz[n])
            a = jnp.exp2(t[:, :RG_BLOCK] * half_sp2[:, cs] + half_sp2[:, cs])
            om = 1.0 - a * a
            mult = om * lax.rsqrt(jnp.maximum(om, 1e-30))
            a_s[rs, cs] = a
            b_s[rs, cs] = mult * ((0.5 * t[:, RG_BLOCK:] + 0.5) * xc[:, cs])

    project(0)
    for p in range(REC_PARTS):
        xc, gz = gate_inputs(p)
        if p + 1 < REC_PARTS:
            project(p + 1)
        recurrence_inputs(p, xc, gz)
    xs[0:REC_HALO, :] = xs[rows:rows + REC_HALO, :]

    def step(t, hc):
        rows_t = pl.ds(pl.multiple_of(t * nb, nb), nb)
        hc = a_s[rows_t, :] * hc + b_s[rows_t, :]
        hs_s[rows_t, :] = hc
        return hc

    carry[...] = lax.fori_loop(0, REC_T, step, carry[...], unroll=8)

    back = []
    for j in range(TIME_BLOCKS):
        rs = slice(j * blk, (j + 1) * blk)
        out = (hs_s[rs, :] * y_s[rs, :]).astype(BF16)
        back.append(jnp.dot(swap, out, preferred_element_type=F32))
    for b in range(nb):
        o_ref[b] = jnp.concatenate(
            [back[j][b * SUBLANES:(b + 1) * SUBLANES] for j in range(TIME_BLOCKS)],
            axis=0).astype(BF16)


def _rec_front(h3, g, w_in, layer, b_in, conv_w, conv_b, w_gate, b_gate, a_param):
    nb, seq, _ = h3.shape
    assert nb == SUBLANES
    rows = nb * REC_T
    return pl.pallas_call(
        _rec_kernel,
        out_shape=jax.ShapeDtypeStruct((nb, seq, D_MODEL), BF16),
        grid=(seq // REC_T,),
        in_specs=[pl.BlockSpec((nb, REC_T, D_MODEL), lambda i: (0, i, 0)),
                  _const_spec((1, D_MODEL)),
                  _layer_spec((D_MODEL, 2 * D_MODEL), layer),
                  _const_spec((1, 2 * D_MODEL)),
                  _const_spec((CONV_WIDTH, D_MODEL)),
                  _const_spec((1, D_MODEL)),
                  _const_spec((N_RG_BLOCKS, RG_BLOCK, 2 * RG_BLOCK)),
                  _const_spec((N_RG_BLOCKS, 1, 2 * RG_BLOCK)),
                  _const_spec((1, D_MODEL))],
        out_specs=pl.BlockSpec((nb, REC_T, D_MODEL), lambda i: (0, i, 0)),
        scratch_shapes=[pltpu.VMEM((REC_HALO + rows, D_MODEL), F32),
                        pltpu.VMEM((rows, D_MODEL), F32),
                        pltpu.VMEM((rows, D_MODEL), F32),
                        pltpu.VMEM((rows, D_MODEL), F32),
                        pltpu.VMEM((rows, D_MODEL), F32),
                        pltpu.VMEM((nb, D_MODEL), F32)],
        compiler_params=_params(("arbitrary",)),
        name="rec_front",
    )(h3, g, w_in, b_in, conv_w, conv_b, w_gate, b_gate, a_param)


def _proj_mlp_kernel(h_ref, m_ref, wo_ref, bo_ref, g_ref, w1_ref, w2_ref, gf_ref, o_ref,
                     *, final_norm):
    h1 = h_ref[...] + jnp.dot(m_ref[...], wo_ref[...], preferred_element_type=F32) + bo_ref[...]
    hn = _rms(h1, g_ref[...]).astype(BF16)
    acc = h1
    for f in range(D_FF // FF_TILE):
        fs = slice(f * FF_TILE, (f + 1) * FF_TILE)
        z = jnp.maximum(jnp.dot(hn, w1_ref[:, fs], preferred_element_type=F32), 0.0)
        acc = acc + jnp.dot((z * z).astype(BF16), w2_ref[fs, :], preferred_element_type=F32)
    if final_norm:
        acc = _rms(acc, gf_ref[...])
    o_ref[...] = acc


def _proj_mlp(h, mix, w_o, mix_layer, b_o, g, w1, w2, layer, g_final, final_norm):
    t = h.shape[0]
    row = lambda i: (i, 0)
    return pl.pallas_call(
        functools.partial(_proj_mlp_kernel, final_norm=final_norm),
        out_shape=jax.ShapeDtypeStruct((t, D_MODEL), F32),
        grid=(t // ROW_TILE,),
        in_specs=[pl.BlockSpec((ROW_TILE, D_MODEL), row),
                  pl.BlockSpec((ROW_TILE, D_MODEL), row),
                  _layer_spec((D_MODEL, D_MODEL), mix_layer),
                  _const_spec((1, D_MODEL)),
                  _const_spec((1, D_MODEL)),
                  _layer_spec((D_MODEL, D_FF), layer),
                  _layer_spec((D_FF, D_MODEL), layer),
                  _const_spec((1, D_MODEL))],
        out_specs=pl.BlockSpec((ROW_TILE, D_MODEL), row),
        compiler_params=_params(("parallel",)),
        name="proj_mlp",
    )(h, mix, w_o, b_o, g, w1, w2, g_final)


def kernel(x, norm_mix, norm_mlp, attn_w_qkv, attn_w_o, attn_rel_bias, rec_w_in, rec_b_in,
           rec_conv_w, rec_conv_b, rec_w_ga, rec_b_ga, rec_w_gx, rec_b_gx, rec_a_param,
           rec_w_o, rec_b_o, mlp_w1, mlp_w2, norm_final):
    batch, seq, _ = x.shape
    depth = norm_mix.shape[0]
    h = x.reshape(batch * seq, D_MODEL)
    row = lambda v: v.reshape(1, -1).astype(F32)
    zero_bias = jnp.zeros((1, D_MODEL), F32)
    w_qkv, w_attn_o = _to_bf16(attn_w_qkv), _to_bf16(attn_w_o)
    w_in, w_rec_o = _to_bf16(rec_w_in), _to_bf16(rec_w_o)
    w1, w2 = _to_bf16(mlp_w1), _to_bf16(mlp_w2)
    for layer in range(depth):
        j = layer // 2
        g_mix = row(norm_mix[layer])
        if layer % 2 == 0:
            q, k, v = _qkv(h, g_mix, w_qkv, j)
            mix = _attn_core(q, k, v, _rel_bias(attn_rel_bias[j]), batch, seq)
            w_o, b_o = w_attn_o, zero_bias
        else:
            w_gate = (0.5 * jnp.concatenate([rec_w_ga[j], rec_w_gx[j]], axis=-1)).astype(BF16)
            b_gate = 0.5 * jnp.concatenate([rec_b_ga[j], rec_b_gx[j]], axis=-1)[:, None, :]
            mix = _rec_front(h.reshape(batch, seq, D_MODEL), g_mix, w_in, j,
                             row(rec_b_in[j]), rec_conv_w[j], row(rec_conv_b[j]),
                             w_gate, b_gate, row(rec_a_param[j]))
            mix = mix.reshape(batch * seq, D_MODEL)
            w_o, b_o = w_rec_o, row(rec_b_o[j])
        h = _proj_mlp(h, mix, w_o, j, b_o, row(norm_mlp[layer]), w1, w2, layer,
                      row(norm_final), final_norm=(layer == depth - 1))
    return h.reshape(batch, seq, D_MODEL)
```

```python
import functools
import math

import numpy as np
import jax
import jax.numpy as jnp
from jax import lax
from jax.experimental import pallas as pl
from jax.experimental.pallas import tpu as pltpu

D_MODEL = 1024
N_HEADS = 16
HEAD_DIM = D_MODEL // N_HEADS
CHUNK = 64
LEFT_CHUNKS = 8
BAND = (LEFT_CHUNKS + 1) * CHUNK
PAD = LEFT_CHUNKS * CHUNK
MAX_REL_DIST = 128
NEG_INF = -1e30
FAR_COLS = (PAD - MAX_REL_DIST) // 128 * 128
N_RG_BLOCKS = 8
RG_BLOCK = D_MODEL // N_RG_BLOCKS
CONV_WIDTH = 4
RG_C = 8.0
D_FF = 4 * D_MODEL
RMS_EPS = 1e-6
LOG2_E = math.log2(math.e)

LANES = 128
SUBLANES = 8
HEADS_PER_VREG = LANES // HEAD_DIM
N_HEAD_PAIRS = N_HEADS // HEADS_PER_VREG
VMEM_LIMIT = 56 * 1024 * 1024

ROW_TILE = 512
CAST_TILE_ELEMS = 1024 * 1024
CHUNKS_PER_STEP = 16
CHUNKS_PER_BODY = 2
SCORE_LOOKAHEAD = 3
REC_T = 64
FF_TILE = 1024

BF16 = jnp.bfloat16
F32 = jnp.float32


def _rms(x, g):
    ms = jnp.mean(x * x, axis=-1, keepdims=True)
    return x * lax.rsqrt(ms + RMS_EPS) * g


def _const_spec(shape):
    zeros = (0,) * len(shape)
    return pl.BlockSpec(shape, lambda *_: zeros)


def _params(sem):
    return pltpu.CompilerParams(dimension_semantics=sem, vmem_limit_bytes=VMEM_LIMIT)


def _cast_kernel(x_ref, o_ref):
    o_ref[...] = x_ref[...].astype(o_ref.dtype)


def _to_bf16(w):
    cols = w.shape[-1]
    flat = w.reshape(-1, cols)
    rows = flat.shape[0]
    tile = 1 << ((CAST_TILE_ELEMS // cols).bit_length() - 1)
    assert rows % tile == 0
    out = pl.pallas_call(
        _cast_kernel,
        out_shape=jax.ShapeDtypeStruct((rows, cols), BF16),
        grid=(rows // tile,),
        in_specs=[pl.BlockSpec((tile, cols), lambda i: (i, 0))],
        out_specs=pl.BlockSpec((tile, cols), lambda i: (i, 0)),
        compiler_params=_params(("parallel",)),
        name="cast_bf16",
    )(flat)
    return out.reshape(w.shape)


def _layer_spec(shape, layer):
    return pl.BlockSpec((None,) + tuple(shape), lambda *_: (layer, 0, 0),
                        pipeline_mode=pl.Buffered(1))


def _qkv_kernel(h_ref, g_ref, wf_ref, q_ref, k_ref, v_ref, w_ref):
    @pl.when(pl.program_id(0) == 0)
    def _():
        for n in range(3):
            cols = slice(n * D_MODEL, (n + 1) * D_MODEL)
            w_ref[:, cols] = wf_ref[:, cols].astype(BF16)

    hn = _rms(h_ref[...], g_ref[...]).astype(BF16)
    q = jnp.dot(hn, w_ref[:, 0:D_MODEL], preferred_element_type=F32)
    q_ref[...] = (q * (HEAD_DIM ** -0.5 * LOG2_E)).astype(BF16)
    k_ref[...] = jnp.dot(hn, w_ref[:, D_MODEL:2 * D_MODEL],
                         preferred_element_type=F32).astype(BF16)
    v_ref[...] = jnp.dot(hn, w_ref[:, 2 * D_MODEL:3 * D_MODEL],
                         preferred_element_type=F32).astype(BF16)


def _qkv(h, g, w, layer):
    t = h.shape[0]
    row = lambda i: (i, 0)
    out = jax.ShapeDtypeStruct((t, D_MODEL), BF16)
    return pl.pallas_call(
        _qkv_kernel,
        out_shape=(out, out, out),
        grid=(t // ROW_TILE,),
        in_specs=[pl.BlockSpec((ROW_TILE, D_MODEL), row),
                  _const_spec((1, D_MODEL)),
                  _layer_spec((D_MODEL, 3 * D_MODEL), layer)],
        out_specs=(pl.BlockSpec((ROW_TILE, D_MODEL), row),) * 3,
        scratch_shapes=[pltpu.VMEM((D_MODEL, 3 * D_MODEL), BF16)],
        compiler_params=_params(("arbitrary",)),
        name="attn_qkv",
    )(h, g, w)


def _attn_kernel(q_ref, k_ref, v_ref, bias_ref, o_ref, kpad, vpad, s_scr, e_scr, l_scr):
    g = pl.program_id(1)
    c0 = g * CHUNKS_PER_STEP

    @pl.when(g == 0)
    def _():
        zeros = jnp.zeros((PAD, D_MODEL), BF16)
        kpad[0:PAD, :] = zeros
        vpad[0:PAD, :] = zeros
        kpad[PAD:, :] = k_ref[...]
        vpad[PAD:, :] = v_ref[...]

    col = lax.broadcasted_iota(jnp.int32, (1, BAND), 1)
    first_head = lax.broadcasted_iota(jnp.int32, (1, LANES), 1) < HEAD_DIM
    mask_a = first_head.astype(F32).astype(BF16)
    mask_b = (1.0 - first_head.astype(F32)).astype(BF16)

    def lanes(hp):
        return slice(hp * LANES, (hp + 1) * LANES)

    def scores(c, qrow, hp, masked):
        q2 = q_ref[pl.ds(qrow, CHUNK), lanes(hp)]
        kb = kpad[pl.ds(pl.multiple_of(c * CHUNK, CHUNK), BAND), lanes(hp)]
        qs = jnp.concatenate([q2 * mask_a, q2 * mask_b], axis=0)
        s = lax.dot_general(qs, kb, (((1,), (1,)), ((), ())), preferred_element_type=F32)
        s = jnp.concatenate([s[:, :FAR_COLS], s[:, FAR_COLS:] + bias_ref[hp]], axis=1)
        if masked:
            s = jnp.where(col >= (LEFT_CHUNKS - c) * CHUNK, s, NEG_INF)
        return s

    def softmax_terms(s):
        m = jnp.max(s, axis=-1, keepdims=True)
        e = jnp.exp2(s - m)
        return e.astype(BF16), jnp.sum(e, axis=-1, keepdims=True)

    def weighted_values(c, qrow, hp, e, l):
        vb = vpad[pl.ds(pl.multiple_of(c * CHUNK, CHUNK), BAND), lanes(hp)]
        o2 = jnp.dot(e, vb, preferred_element_type=F32) / l
        o = jnp.where(first_head, o2[:CHUNK], o2[CHUNK:])
        o_ref[pl.ds(qrow, CHUNK), lanes(hp)] = o.astype(BF16)

    def stages(first, masked, deferred):
        def at(idx):
            idx = idx if isinstance(idx, int) else jnp.minimum(idx, CHUNKS_PER_STEP - 1)
            row = idx * CHUNK
            return c0 + idx, row if isinstance(row, int) else pl.multiple_of(row, CHUNK)

        todo = [at(first + i) + (hp,) for i in range(CHUNKS_PER_BODY)
                for hp in range(N_HEAD_PAIRS)]
        after = at(first + CHUNKS_PER_BODY)
        if deferred:
            weighted_values(*at(first - 1), N_HEAD_PAIRS - 1, e_scr[...], l_scr[...])
        ahead = {}
        for i, (c, qrow, hp) in enumerate(todo):
            j = i + SCORE_LOOKAHEAD
            if j < len(todo):
                ahead[j] = scores(*todo[j], masked)
            s = s_scr[i] if i < SCORE_LOOKAHEAD else ahead.pop(i)
            if j >= len(todo):
                s_scr[j - len(todo)] = scores(*after, j - len(todo), masked)
            e, l = softmax_terms(s)
            if i == len(todo) - 1:
                e_scr[...] = e
                l_scr[...] = l
            else:
                weighted_values(c, qrow, hp, e, l)

    def body(masked, bb, carry):
        stages(bb * CHUNKS_PER_BODY, masked, True)
        return carry

    n_bodies = CHUNKS_PER_STEP // CHUNKS_PER_BODY
    n_masked = jnp.clip(pl.cdiv(LEFT_CHUNKS - c0, CHUNKS_PER_BODY), 1, n_bodies)
    for hp in range(SCORE_LOOKAHEAD):
        s_scr[hp] = scores(c0, 0, hp, True)
    stages(0, True, False)
    lax.fori_loop(1, n_masked, functools.partial(body, True), 0)
    lax.fori_loop(n_masked, n_bodies, functools.partial(body, False), 0)
    weighted_values(c0 + CHUNKS_PER_STEP - 1, (CHUNKS_PER_STEP - 1) * CHUNK,
                    N_HEAD_PAIRS - 1, e_scr[...], l_scr[...])


def _attn_core(q, k, v, bias2, batch, seq):
    rows = CHUNKS_PER_STEP * CHUNK
    steps = seq // rows
    tok = lambda b, g: (b * steps + g, 0)
    whole_batch = lambda b, g: (b, 0)
    return pl.pallas_call(
        _attn_kernel,
        out_shape=jax.ShapeDtypeStruct((batch * seq, D_MODEL), BF16),
        grid=(batch, steps),
        in_specs=[pl.BlockSpec((rows, D_MODEL), tok),
                  pl.BlockSpec((seq, D_MODEL), whole_batch),
                  pl.BlockSpec((seq, D_MODEL), whole_batch),
                  _const_spec((N_HEAD_PAIRS, HEADS_PER_VREG * CHUNK, BAND - FAR_COLS))],
        out_specs=pl.BlockSpec((rows, D_MODEL), tok),
        scratch_shapes=[pltpu.VMEM((PAD + seq, D_MODEL), BF16),
                        pltpu.VMEM((PAD + seq, D_MODEL), BF16),
                        pltpu.VMEM((SCORE_LOOKAHEAD, HEADS_PER_VREG * CHUNK, BAND), F32),
                        pltpu.VMEM((HEADS_PER_VREG * CHUNK, BAND), BF16),
                        pltpu.VMEM((HEADS_PER_VREG * CHUNK, 1), F32)],
        compiler_params=_params(("parallel", "arbitrary")),
        name="attn_core",
    )(q, k, v, bias2)


def _rel_bias(rel_table):
    period = BAND + CHUNK
    t = np.arange(period)
    rel = PAD - np.where(t <= BAND, t, t - period)
    idx = np.clip(rel, -MAX_REL_DIST, MAX_REL_DIST) + MAX_REL_DIST
    row = rel_table[:, idx].astype(F32)
    row = (row - row[:, 0:1]) * LOG2_E
    bias = jnp.tile(row, (1, CHUNK))[:, :CHUNK * (period - 1)]
    bias = bias.reshape(N_HEADS, CHUNK, period - 1)[:, :, FAR_COLS:BAND]
    return bias.reshape(N_HEAD_PAIRS, HEADS_PER_VREG * CHUNK, BAND - FAR_COLS)


REC_HALO = (CONV_WIDTH - 1) * SUBLANES
TIME_BLOCKS = REC_T // SUBLANES
REC_PARTS = 2


def _rec_kernel(h_ref, g_ref, winf_ref, bin_ref, cw_ref, cb_ref, wg_ref, bg_ref,
                ap_ref, o_ref, xs, a_s, b_s, hs_s, y_s, carry, win_ref):
    nb = h_ref.shape[0]
    rows = nb * REC_T
    blk = nb * SUBLANES
    i = pl.program_id(0)

    @pl.when(i == 0)
    def _():
        xs[0:REC_HALO, :] = jnp.zeros((REC_HALO, D_MODEL), F32)
        carry[...] = jnp.zeros_like(carry)
        for n in range(2):
            cols = slice(n * D_MODEL, (n + 1) * D_MODEL)
            win_ref[:, cols] = winf_ref[:, cols].astype(BF16)

    r = lax.broadcasted_iota(jnp.int32, (blk, blk), 0)
    c = lax.broadcasted_iota(jnp.int32, (blk, blk), 1)
    hi, lo = SUBLANES.bit_length() - 1, SUBLANES - 1
    swap = jnp.where(((r >> hi) == (c & lo)) & ((r & lo) == (c >> hi)), 1.0, 0.0).astype(BF16)

    x = h_ref[...].reshape(rows, D_MODEL)
    ap = ap_ref[...]
    neg_sp = -RG_C * (jnp.maximum(-ap, 0.0) + jnp.log1p(jnp.exp(-jnp.abs(ap))))
    half_sp2 = (0.5 * LOG2_E) * neg_sp
    part_rows = rows // REC_PARTS
    part_blocks = TIME_BLOCKS // REC_PARTS

    def project(p):
        parts = []
        for j in range(p * part_blocks, (p + 1) * part_blocks):
            group = jnp.concatenate(
                [x[b * REC_T + j * SUBLANES:b * REC_T + (j + 1) * SUBLANES] for b in range(nb)],
                axis=0)
            group = _rms(group, g_ref[...]).astype(BF16)
            parts.append(jnp.dot(swap, group, preferred_element_type=F32).astype(BF16))
        hn = jnp.concatenate(parts, axis=0)
        u = jnp.dot(hn, win_ref[...], preferred_element_type=F32) + bin_ref[...]
        rs = slice(p * part_rows, (p + 1) * part_rows)
        xs[REC_HALO + p * part_rows:REC_HALO + (p + 1) * part_rows, :] = u[:, :D_MODEL]
        y_s[rs, :] = u[:, D_MODEL:]

    def gate_inputs(p):
        r0 = p * part_rows
        rs = slice(r0, r0 + part_rows)
        y_s[rs, :] = jax.nn.gelu(y_s[rs, :], approximate=True)
        xc = cb_ref[...] + xs[r0:r0 + part_rows, :] * cw_ref[0:1, :]
        for tap in range(1, CONV_WIDTH):
            xc = xc + xs[r0 + tap * nb:r0 + tap * nb + part_rows, :] * cw_ref[tap:tap + 1, :]
        xcb = xc.astype(BF16)
        gz = [jnp.dot(xcb[:, n * RG_BLOCK:(n + 1) * RG_BLOCK], wg_ref[n],
                      preferred_element_type=F32) + bg_ref[n] for n in range(N_RG_BLOCKS)]
        return xc, gz

    def recurrence_inputs(p, xc, gz):
        rs = slice(p * part_rows, (p + 1) * part_rows)
        for n in range(N_RG_BLOCKS):
            cs = slice(n * RG_BLOCK, (n + 1) * RG_BLOCK)
            t = jnp.tanh(gz[n])
            a = jnp.exp2(t[:, :RG_BLOCK] * half_sp2[:, cs] + half_sp2[:, cs])
            om = 1.0 - a * a
            mult = om * lax.rsqrt(jnp.maximum(om, 1e-30))
            a_s[rs, cs] = a
            b_s[rs, cs] = mult * ((0.5 * t[:, RG_BLOCK:] + 0.5) * xc[:, cs])

    project(0)
    for p in range(REC_PARTS):
        xc, gz = gate_inputs(p)
        if p + 1 < REC_PARTS:
            project(p + 1)
        recurrence_inputs(p, xc, gz)
    xs[0:REC_HALO, :] = xs[rows:rows + REC_HALO, :]

    def step(t, hc):
        rows_t = pl.ds(pl.multiple_of(t * nb, nb), nb)
        hc = a_s[rows_t, :] * hc + b_s[rows_t, :]
        hs_s[rows_t, :] = hc
        return hc

    carry[...] = lax.fori_loop(0, REC_T, step, carry[...], unroll=8)

    back = []
    for j in range(TIME_BLOCKS):
        rs = slice(j * blk, (j + 1) * blk)
        out = (hs_s[rs, :] * y_s[rs, :]).astype(BF16)
        back.append(jnp.dot(swap, out, preferred_element_type=F32))
    for b in range(nb):
        o_ref[b] = jnp.concatenate(
            [back[j][b * SUBLANES:(b + 1) * SUBLANES] for j in range(TIME_BLOCKS)],
            axis=0).astype(BF16)


def _rec_front(h3, g, w_in, layer, b_in, conv_w, conv_b, w_gate, b_gate, a_param):
    nb, seq, _ = h3.shape
    assert nb == SUBLANES
    rows = nb * REC_T
    return pl.pallas_call(
        _rec_kernel,
        out_shape=jax.ShapeDtypeStruct((nb, seq, D_MODEL), BF16),
        grid=(seq // REC_T,),
        in_specs=[pl.BlockSpec((nb, REC_T, D_MODEL), lambda i: (0, i, 0)),
                  _const_spec((1, D_MODEL)),
                  _layer_spec((D_MODEL, 2 * D_MODEL), layer),
                  _const_spec((1, 2 * D_MODEL)),
                  _const_spec((CONV_WIDTH, D_MODEL)),
                  _const_spec((1, D_MODEL)),
                  _const_spec((N_RG_BLOCKS, RG_BLOCK, 2 * RG_BLOCK)),
                  _const_spec((N_RG_BLOCKS, 1, 2 * RG_BLOCK)),
                  _const_spec((1, D_MODEL))],
        out_specs=pl.BlockSpec((nb, REC_T, D_MODEL), lambda i: (0, i, 0)),
        scratch_shapes=[pltpu.VMEM((REC_HALO + rows, D_MODEL), F32),
                        pltpu.VMEM((rows, D_MODEL), F32),
                        pltpu.VMEM((rows, D_MODEL), F32),
                        pltpu.VMEM((rows, D_MODEL), F32),
                        pltpu.VMEM((rows, D_MODEL), F32),
                        pltpu.VMEM((nb, D_MODEL), F32),
                        pltpu.VMEM((D_MODEL, 2 * D_MODEL), BF16)],
        compiler_params=_params(("arbitrary",)),
        name="rec_front",
    )(h3, g, w_in, b_in, conv_w, conv_b, w_gate, b_gate, a_param)


def _proj_mlp_kernel(h_ref, m_ref, wo_ref, bo_ref, g_ref, w1_ref, w2_ref, gf_ref, o_ref,
                     *, final_norm):
    h1 = h_ref[...] + jnp.dot(m_ref[...], wo_ref[...], preferred_element_type=F32) + bo_ref[...]
    hn = _rms(h1, g_ref[...]).astype(BF16)
    acc = h1
    for f in range(D_FF // FF_TILE):
        fs = slice(f * FF_TILE, (f + 1) * FF_TILE)
        z = jnp.maximum(jnp.dot(hn, w1_ref[:, fs], preferred_element_type=F32), 0.0)
        acc = acc + jnp.dot((z * z).astype(BF16), w2_ref[fs, :], preferred_element_type=F32)
    if final_norm:
        acc = _rms(acc, gf_ref[...])
    o_ref[...] = acc


def _proj_mlp(h, mix, w_o, mix_layer, b_o, g, w1, w2, layer, g_final, final_norm):
    t = h.shape[0]
    row = lambda i: (i, 0)
    return pl.pallas_call(
        functools.partial(_proj_mlp_kernel, final_norm=final_norm),
        out_shape=jax.ShapeDtypeStruct((t, D_MODEL), F32),
        grid=(t // ROW_TILE,),
        in_specs=[pl.BlockSpec((ROW_TILE, D_MODEL), row),
                  pl.BlockSpec((ROW_TILE, D_MODEL), row),
                  _layer_spec((D_MODEL, D_MODEL), mix_layer),
                  _const_spec((1, D_MODEL)),
                  _const_spec((1, D_MODEL)),
                  _layer_spec((D_MODEL, D_FF), layer),
                  _layer_spec((D_FF, D_MODEL), layer),
                  _const_spec((1, D_MODEL))],
        out_specs=pl.BlockSpec((ROW_TILE, D_MODEL), row),
        compiler_params=_params(("parallel",)),
        name="proj_mlp",
    )(h, mix, w_o, b_o, g, w1, w2, g_final)


def kernel(x, norm_mix, norm_mlp, attn_w_qkv, attn_w_o, attn_rel_bias, rec_w_in, rec_b_in,
           rec_conv_w, rec_conv_b, rec_w_ga, rec_b_ga, rec_w_gx, rec_b_gx, rec_a_param,
           rec_w_o, rec_b_o, mlp_w1, mlp_w2, norm_final):
    batch, seq, _ = x.shape
    depth = norm_mix.shape[0]
    h = x.reshape(batch * seq, D_MODEL)
    row = lambda v: v.reshape(1, -1).astype(F32)
    zero_bias = jnp.zeros((1, D_MODEL), F32)
    w_qkv, w_attn_o = attn_w_qkv, _to_bf16(attn_w_o)
    w_in, w_rec_o = rec_w_in, _to_bf16(rec_w_o)
    w1, w2 = _to_bf16(mlp_w1), _to_bf16(mlp_w2)
    for layer in range(depth):
        j = layer // 2
        g_mix = row(norm_mix[layer])
        if layer % 2 == 0:
            q, k, v = _qkv(h, g_mix, w_qkv, j)
            mix = _attn_core(q, k, v, _rel_bias(attn_rel_bias[j]), batch, seq)
            w_o, b_o = w_attn_o, zero_bias
        else:
            w_gate = (0.5 * jnp.concatenate([rec_w_ga[j], rec_w_gx[j]], axis=-1)).astype(BF16)
            b_gate = 0.5 * jnp.concatenate([rec_b_ga[j], rec_b_gx[j]], axis=-1)[:, None, :]
            mix = _rec_front(h.reshape(batch, seq, D_MODEL), g_mix, w_in, j,
                             row(rec_b_in[j]), rec_conv_w[j], row(rec_conv_b[j]),
                             w_gate, b_gate, row(rec_a_param[j]))
            mix = mix.reshape(batch * seq, D_MODEL)
            w_o, b_o = w_rec_o, row(rec_b_o[j])
        h = _proj_mlp(h, mix, w_o, j, b_o, row(norm_mlp[layer]), w1, w2, layer,
                      row(norm_final), final_norm=(layer == depth - 1))
    return h.reshape(batch, seq, D_MODEL)
```
